```python
import jax, jax.numpy as jnp
from jax import lax
import numpy as np

D_MODEL = 1024
BATCH = 32
SEQ = 2048
DEPTH = 1

MEM_LEN = 256
CONV_WIDTH = 31
CONV_DIM = 1024
SGU_DIM = 1024
SGU_GROUPS = 8
SGU_CHUNK = 128
XATTN_HEADS = 4
XATTN_HEAD_DIM = D_MODEL // XATTN_HEADS
FFN_HIDDEN = ((-(-8 * D_MODEL // 3) + 255) // 256) * 256
IN_COLS = 2 * CONV_DIM + 2 * SGU_DIM + 2 * D_MODEL
RMS_EPS = 1e-6
LN_EPS = 1e-5

kernel_name = "hybrid_conv_sgu_gated_block"


def rmsnorm(x, g):
    xf = x.astype(jnp.float32)
    y = xf * lax.rsqrt(jnp.mean(xf * xf, axis=-1, keepdims=True) + RMS_EPS)
    return (y * g.astype(jnp.float32)).astype(x.dtype)


def layernorm(x, g, b):
    xf = x.astype(jnp.float32)
    mu = jnp.mean(xf, axis=-1, keepdims=True)
    var = jnp.mean(jnp.square(xf - mu), axis=-1, keepdims=True)
    y = (xf - mu) * lax.rsqrt(var + LN_EPS) * g.astype(jnp.float32) + b.astype(jnp.float32)
    return y.astype(x.dtype)


def causal_depthwise_conv(x, w, b):
    y = lax.conv_general_dilated(
        x, w[:, None, :], window_strides=(1,), padding=[(CONV_WIDTH - 1, 0)],
        dimension_numbers=("NWC", "WIO", "NWC"), feature_group_count=x.shape[-1])
    return y + b


def chunked_spatial_gating(u, v, w_s, b_s):
    B, S, C = v.shape
    n_chunks = S // SGU_CHUNK
    gd = C // SGU_GROUPS
    mask = jnp.tril(jnp.ones((SGU_CHUNK, SGU_CHUNK), dtype=bool))
    w = jnp.where(mask[None], w_s, jnp.zeros_like(w_s))
    vc = v.reshape(B, n_chunks, SGU_CHUNK, SGU_GROUPS, gd)
    z = jnp.einsum('gts,bnsgc->bntgc', w, vc) + jnp.transpose(b_s)[None, None, :, :, None]
    return u * z.reshape(B, S, C)


def mixer_block(h, w_in, b_gate, conv_w, conv_b, conv_ln_g, conv_ln_b, w_conv_out,
                sgu_ln_g, sgu_ln_b, sgu_w, sgu_b, w_sgu_out, w_mix_out):
    p = jnp.einsum('bsd,de->bse', h, w_in)
    a_val, a_gate, b_u, b_v, g_a, g_b = jnp.split(
        p, np.cumsum([CONV_DIM, CONV_DIM, SGU_DIM, SGU_DIM, D_MODEL]).tolist(), axis=-1)
    a = a_val * jax.nn.sigmoid(a_gate)
    a = causal_depthwise_conv(a, conv_w, conv_b)
    a = jax.nn.silu(layernorm(a, conv_ln_g, conv_ln_b))
    y_a = jnp.einsum('bsc,cd->bsd', a, w_conv_out)
    u = jax.nn.gelu(b_u)
    v = layernorm(jax.nn.gelu(b_v), sgu_ln_g, sgu_ln_b)
    y_b = jnp.einsum('bsc,cd->bsd', chunked_spatial_gating(u, v, sgu_w, sgu_b), w_sgu_out)
    merged = jax.nn.sigmoid(g_a + b_gate[0]) * y_a + jax.nn.sigmoid(g_b + b_gate[1]) * y_b
    return jnp.einsum('bsd,de->bse', merged, w_mix_out)


def memory_cross_attention(h, mem_n, w_q, w_kv, w_xo):
    B, S, _ = h.shape
    M = mem_n.shape[1]
    q = jnp.einsum('bsd,de->bse', h, w_q).reshape(B, S, XATTN_HEADS, XATTN_HEAD_DIM)
    kv = jnp.einsum('bmd,de->bme', mem_n, w_kv)
    k, v = jnp.split(kv, 2, axis=-1)
    k = k.reshape(B, M, XATTN_HEADS, XATTN_HEAD_DIM)
    v = v.reshape(B, M, XATTN_HEADS, XATTN_HEAD_DIM)
    s = jnp.einsum('bshd,bmhd->bhsm', q, k).astype(jnp.float32) * (XATTN_HEAD_DIM ** -0.5)
    pr = jax.nn.softmax(s, axis=-1).astype(v.dtype)
    o = jnp.einsum('bhsm,bmhd->bshd', pr, v).reshape(B, S, D_MODEL)
    return jnp.einsum('bsd,de->bse', o, w_xo)


def swiglu_ffn(h, w_gu, w_down):
    gu = jnp.einsum('bsd,df->bsf', h, w_gu)
    gt, up = jnp.split(gu, 2, axis=-1)
    return jnp.einsum('bsf,fd->bsd', jax.nn.silu(gt) * up, w_down)


def setup_inputs(seed: int = 0) -> dict:
    key = jax.random.key(seed)
    ks = jax.random.split(key, 32)
    L, D = DEPTH, D_MODEL
    f32 = jnp.float32

    def nrm(k, shape, scale):
        return jax.random.normal(k, shape, f32) * scale

    def gain(k, shape):
        return 1.0 + 0.02 * jax.random.normal(k, shape, f32)

    return {
        "x": jax.random.normal(ks[0], (BATCH, SEQ, D), f32),
        "mem": jax.random.normal(ks[1], (BATCH, MEM_LEN, D), f32),
        "norm_mix": gain(ks[2], (L, D)),
        "w_in": nrm(ks[3], (L, D, IN_COLS), D ** -0.5),
        "b_gate": nrm(ks[4], (L, 2, D), 0.02),
        "conv_w": nrm(ks[5], (L, CONV_WIDTH, CONV_DIM), CONV_WIDTH ** -0.5),
        "conv_b": nrm(ks[6], (L, CONV_DIM), 0.02),
        "conv_ln_g": gain(ks[7], (L, CONV_DIM)),
        "conv_ln_b": nrm(ks[8], (L, CONV_DIM), 0.02),
        "w_conv_out": nrm(ks[9], (L, CONV_DIM, D), CONV_DIM ** -0.5),
        "sgu_ln_g": gain(ks[10], (L, SGU_DIM)),
        "sgu_ln_b": nrm(ks[11], (L, SGU_DIM), 0.02),
        "sgu_w": nrm(ks[12], (L, SGU_GROUPS, SGU_CHUNK, SGU_CHUNK), SGU_CHUNK ** -0.5),
        "sgu_b": gain(ks[13], (L, SGU_GROUPS, SGU_CHUNK)),
        "w_sgu_out": nrm(ks[14], (L, SGU_DIM, D), SGU_DIM ** -0.5),
        "w_mix_out": nrm(ks[15], (L, D, D), D ** -0.5),
        "norm_xattn": gain(ks[16], (L, D)),
        "norm_mem": gain(ks[17], (L, D)),
        "w_q": nrm(ks[18], (L, D, D), D ** -0.5),
        "w_kv": nrm(ks[19], (L, D, 2 * D), D ** -0.5),
        "w_xo": nrm(ks[20], (L, D, D), D ** -0.5),
        "norm_ffn": gain(ks[21], (L, D)),
        "w_gu": nrm(ks[22], (L, D, 2 * FFN_HIDDEN), D ** -0.5),
        "w_down": nrm(ks[23], (L, FFN_HIDDEN, D), FFN_HIDDEN ** -0.5),
        "norm_final": gain(ks[24], (D,)),
    }


def reference(x, mem, norm_mix, w_in, b_gate, conv_w, conv_b, conv_ln_g, conv_ln_b, w_conv_out,
              sgu_ln_g, sgu_ln_b, sgu_w, sgu_b, w_sgu_out, w_mix_out,
              norm_xattn, norm_mem, w_q, w_kv, w_xo,
              norm_ffn, w_gu, w_down, norm_final):
    for l in range(DEPTH):
        h = rmsnorm(x, norm_mix[l])
        x = x + mixer_block(h, w_in[l], b_gate[l], conv_w[l], conv_b[l], conv_ln_g[l], conv_ln_b[l],
                            w_conv_out[l], sgu_ln_g[l], sgu_ln_b[l], sgu_w[l], sgu_b[l],
                            w_sgu_out[l], w_mix_out[l])
        h = rmsnorm(x, norm_xattn[l])
        mem_n = rmsnorm(mem, norm_mem[l])
        x = x + memory_cross_attention(h, mem_n, w_q[l], w_kv[l], w_xo[l])
        h = rmsnorm(x, norm_ffn[l])
        x = x + swiglu_ffn(h, w_gu[l], w_down[l])
    return rmsnorm(x, norm_final)
```

```python
import functools

import jax
import jax.numpy as jnp
from jax import lax
from jax.experimental import pallas as pl
from jax.experimental.pallas import tpu as pltpu

CONV_WIDTH = 31
SGU_GROUPS = 8
SGU_CHUNK = 128
XATTN_HEADS = 4
RMS_EPS = 1e-6
LN_EPS = 1e-5

SUBLANES = 8
LANES = 128
CONV_HALO = -(-(CONV_WIDTH - 1) // SUBLANES) * SUBLANES
VMEM_LIMIT_BYTES = 56 * 1024 * 1024

SEQ_TILE_MIXER = 256
SEQ_TILE_XATTN = 512
SEQ_TILE_FFN = 512
FFN_HIDDEN_CHUNKS = 2

_F32 = jnp.float32
_BF16 = jnp.bfloat16


def _dot(a, b):
    return jnp.dot(a, b, preferred_element_type=_F32)


def _rmsnorm(x, g):
    return x * lax.rsqrt(jnp.mean(x * x, axis=-1, keepdims=True) + RMS_EPS) * g


def _layernorm(x, g, b):
    mu = jnp.mean(x, axis=-1, keepdims=True)
    xc = x - mu
    var = jnp.mean(xc * xc, axis=-1, keepdims=True)
    return xc * lax.rsqrt(var + LN_EPS) * g + b


def _mixer_kernel(x_ref, nrm_ref, w_in_ref, bgate_ref, convw_ref, convb_ref, clng_ref, clnb_ref,
                  wco_ref, slng_ref, slnb_ref, sguw_ref, sgub_ref, wso_ref, wmo_ref,
                  o_ref, abuf_ref, v_ref, sg_ref):
    seq_tile, d = x_ref.shape
    x = x_ref[...]
    h = _rmsnorm(x, nrm_ref[...]).astype(_BF16)

    def proj(i):
        return _dot(h, w_in_ref[:, i * d:(i + 1) * d])

    @pl.when(pl.program_id(1) == 0)
    def _():
        abuf_ref[0:CONV_HALO, :] = jnp.zeros((CONV_HALO, d), _F32)

    abuf_ref[CONV_HALO:CONV_HALO + seq_tile, :] = proj(0) * jax.nn.sigmoid(proj(1))
    conv = jnp.broadcast_to(convb_ref[...], (seq_tile, d))
    for k in range(CONV_WIDTH):
        start = CONV_HALO - (CONV_WIDTH - 1) + k
        conv = conv + convw_ref[k:k + 1, :] * abuf_ref[start:start + seq_tile, :]
    abuf_ref[0:CONV_HALO, :] = abuf_ref[seq_tile:seq_tile + CONV_HALO, :]
    a = jax.nn.silu(_layernorm(conv, clng_ref[...], clnb_ref[...])).astype(_BF16)
    y_a = _dot(a, wco_ref[...])

    u = jax.nn.gelu(proj(2))
    v_ref[...] = _layernorm(jax.nn.gelu(proj(3)), slng_ref[...], slnb_ref[...]).astype(_BF16)
    n_chunks = seq_tile // SGU_CHUNK
    gd = d // SGU_GROUPS
    row = lax.broadcasted_iota(jnp.int32, (SGU_CHUNK, SGU_CHUNK), 0)
    col = lax.broadcasted_iota(jnp.int32, (SGU_CHUNK, SGU_CHUNK), 1)
    causal = col <= row
    for g in range(SGU_GROUPS):
        w_g = jnp.where(causal, sguw_ref[g], 0.0).astype(_BF16)
        v_g = jnp.concatenate(
            [v_ref[n * SGU_CHUNK:(n + 1) * SGU_CHUNK, g * gd:(g + 1) * gd] for n in range(n_chunks)], axis=1)
        z_g = _dot(w_g, v_g) + sgub_ref[:, g:g + 1]
        for n in range(n_chunks):
            sg_ref[n * SGU_CHUNK:(n + 1) * SGU_CHUNK, g * gd:(g + 1) * gd] = z_g[:, n * gd:(n + 1) * gd]
    y_b = _dot((u * sg_ref[...]).astype(_BF16), wso_ref[...])

    merged = (jax.nn.sigmoid(proj(4) + bgate_ref[0:1, :]) * y_a
              + jax.nn.sigmoid(proj(5) + bgate_ref[1:2, :]) * y_b)
    o_ref[...] = x + _dot(merged.astype(_BF16), wmo_ref[...])


def _xattn_kernel(x_ref, mem_ref, nx_ref, nm_ref, wq_ref, wkv_ref, wxo_ref, o_ref, k_ref, v_ref):
    d = x_ref.shape[-1]
    hd = d // XATTN_HEADS

    @pl.when(pl.program_id(1) == 0)
    def _():
        mem_n = _rmsnorm(mem_ref[...], nm_ref[...]).astype(_BF16)
        kv = _dot(mem_n, wkv_ref[...])
        k_ref[...] = kv[:, :d].astype(_BF16)
        v_ref[...] = kv[:, d:].astype(_BF16)

    x = x_ref[...]
    h = _rmsnorm(x, nx_ref[...]).astype(_BF16)
    q = (_dot(h, wq_ref[...]) * (hd ** -0.5)).astype(_BF16)
    heads = []
    for i in range(XATTN_HEADS):
        sl = slice(i * hd, (i + 1) * hd)
        s = lax.dot_general(q[:, sl], k_ref[:, sl], (((1,), (1,)), ((), ())), preferred_element_type=_F32)
        e = jnp.exp(s - jnp.max(s, axis=-1, keepdims=True))
        p = e / jnp.sum(e, axis=-1, keepdims=True)
        heads.append(_dot(p.astype(_BF16), v_ref[:, sl]).astype(_BF16))
    o_ref[...] = x + _dot(jnp.concatenate(heads, axis=1), wxo_ref[...])


def _ffn_kernel(x_ref, nf_ref, wgu_ref, wdn_ref, nfin_ref, o_ref, *, apply_final_norm):
    f = wdn_ref.shape[0]
    fc = f // FFN_HIDDEN_CHUNKS
    x = x_ref[...]
    h = _rmsnorm(x, nf_ref[...]).astype(_BF16)
    y = x
    for c in range(FFN_HIDDEN_CHUNKS):
        gt = _dot(h, wgu_ref[:, c * fc:(c + 1) * fc])
        up = _dot(h, wgu_ref[:, f + c * fc:f + (c + 1) * fc])
        y = y + _dot((jax.nn.silu(gt) * up).astype(_BF16), wdn_ref[c * fc:(c + 1) * fc, :])
    if apply_final_norm:
        y = _rmsnorm(y, nfin_ref[...])
    o_ref[...] = y


def _resident(shape):
    return pl.BlockSpec(shape, lambda b, s: (0,) * len(shape), pipeline_mode=pl.Buffered(1))


def _seq_tiled_call(body, x, seq_tile, resident_args, extra_specs=(), extra_args=(), scratch_shapes=()):
    batch, seq, d = x.shape
    assert seq % seq_tile == 0
    x_spec = pl.BlockSpec((None, seq_tile, d), lambda b, s: (b, s, 0))
    return pl.pallas_call(
        body,
        out_shape=jax.ShapeDtypeStruct(x.shape, x.dtype),
        grid=(batch, seq // seq_tile),
        in_specs=[x_spec, *extra_specs, *[_resident(a.shape) for a in resident_args]],
        out_specs=x_spec,
        scratch_shapes=list(scratch_shapes),
        compiler_params=pltpu.CompilerParams(
            dimension_semantics=("parallel", "arbitrary"), vmem_limit_bytes=VMEM_LIMIT_BYTES),
    )(x, *extra_args, *resident_args)


def _row(v):
    return v.reshape(1, -1)


def kernel(x, mem, norm_mix, w_in, b_gate, conv_w, conv_b, conv_ln_g, conv_ln_b, w_conv_out, sgu_ln_g, sgu_ln_b, sgu_w, sgu_b, w_sgu_out, w_mix_out, norm_xattn, norm_mem, w_q, w_kv, w_xo, norm_ffn, w_gu, w_down, norm_final):
    batch, seq, d = x.shape
    mem_len = mem.shape[1]
    depth = w_in.shape[0]
    assert d % (SGU_GROUPS * LANES) == 0 and d % (XATTN_HEADS * LANES) == 0
    assert SEQ_TILE_MIXER % SGU_CHUNK == 0 and SEQ_TILE_MIXER >= CONV_HALO
    assert w_down.shape[1] % (FFN_HIDDEN_CHUNKS * LANES) == 0
    bf = lambda w: w.astype(_BF16)

    for l in range(depth):
        x = _seq_tiled_call(
            _mixer_kernel, x, SEQ_TILE_MIXER,
            [_row(norm_mix[l]), bf(w_in[l]), b_gate[l], conv_w[l], _row(conv_b[l]), _row(conv_ln_g[l]),
             _row(conv_ln_b[l]), bf(w_conv_out[l]), _row(sgu_ln_g[l]), _row(sgu_ln_b[l]), sgu_w[l],
             jnp.transpose(sgu_b[l]), bf(w_sgu_out[l]), bf(w_mix_out[l])],
            scratch_shapes=[pltpu.VMEM((CONV_HALO + SEQ_TILE_MIXER, d), _F32),
                            pltpu.VMEM((SEQ_TILE_MIXER, d), _BF16),
                            pltpu.VMEM((SEQ_TILE_MIXER, d), _F32)])
        x = _seq_tiled_call(
            _xattn_kernel, x, SEQ_TILE_XATTN,
            [_row(norm_xattn[l]), _row(norm_mem[l]), bf(w_q[l]), bf(w_kv[l]), bf(w_xo[l])],
            extra_specs=[pl.BlockSpec((None, mem_len, d), lambda b, s: (b, 0, 0))], extra_args=[mem],
            scratch_shapes=[pltpu.VMEM((mem_len, d), _BF16), pltpu.VMEM((mem_len, d), _BF16)])
        x = _seq_tiled_call(
            functools.partial(_ffn_kernel, apply_final_norm=(l == depth - 1)), x, SEQ_TILE_FFN,
            [_row(norm_ffn[l]), bf(w_gu[l]), bf(w_down[l]), _row(norm_final)])
    return x
```

```python
import functools

import jax
import jax.numpy as jnp
from jax import lax
from jax.experimental import pallas as pl
from jax.experimental.pallas import tpu as pltpu

CONV_WIDTH = 31
SGU_GROUPS = 8
SGU_CHUNK = 128
XATTN_HEADS = 4
RMS_EPS = 1e-6
LN_EPS = 1e-5

SUBLANES = 8
LANES = 128
CONV_HALO = -(-(CONV_WIDTH - 1) // SUBLANES) * SUBLANES
VMEM_LIMIT_BYTES = 56 * 1024 * 1024

SEQ_TILE_MIXER = 256
SEQ_TILE_XATTN = 512
SEQ_TILE_FFN = 512
FFN_HIDDEN_CHUNKS = 2

_F32 = jnp.float32
_BF16 = jnp.bfloat16


def _dot(a, b):
    return jnp.dot(a, b, preferred_element_type=_F32)


def _lane_blocked(w):
    rows, cols = w.shape
    return jnp.transpose(w.astype(_BF16).reshape(rows, cols // LANES, LANES), (1, 0, 2))


def _wslice(w_ref, col_start, col_size, row_start=0, row_size=None):
    row_size = w_ref.shape[1] if row_size is None else row_size
    blocks = range(col_start // LANES, (col_start + col_size) // LANES)
    return jnp.concatenate([w_ref[j, row_start:row_start + row_size, :] for j in blocks], axis=1)


def _rmsnorm(x, g):
    return x * lax.rsqrt(jnp.mean(x * x, axis=-1, keepdims=True) + RMS_EPS) * g


def _layernorm(x, g, b):
    mu = jnp.mean(x, axis=-1, keepdims=True)
    xc = x - mu
    var = jnp.mean(xc * xc, axis=-1, keepdims=True)
    return xc * lax.rsqrt(var + LN_EPS) * g + b


def _causal_depthwise_conv(abuf_ref, shift_ref, convw_ref, bias, seq_tile):
    n_rows = CONV_HALO + seq_tile - SUBLANES
    for r in range(1, SUBLANES):
        shift_ref[r - 1, SUBLANES:SUBLANES + n_rows, :] = abuf_ref[SUBLANES - r:SUBLANES - r + n_rows, :]
    conv = bias
    for delay in range(CONV_WIDTH):
        q, r = divmod(delay, SUBLANES)
        k = CONV_WIDTH - 1 - delay
        start = CONV_HALO - q * SUBLANES
        if r == 0:
            src = abuf_ref[start:start + seq_tile, :]
        else:
            src = shift_ref[r - 1, start:start + seq_tile, :]
        conv = conv + convw_ref[k:k + 1, :] * src
    return conv


def _mixer_kernel(x_ref, nrm_ref, w_in_ref, bgate_ref, convw_ref, convb_ref, clng_ref, clnb_ref,
                  wco_ref, slng_ref, slnb_ref, sguw_ref, sgub_ref, wso_ref, wmo_ref,
                  o_ref, abuf_ref, shift_ref, v_ref, sg_ref):
    seq_tile, d = x_ref.shape
    x = x_ref[...]
    h = _rmsnorm(x, nrm_ref[...]).astype(_BF16)

    def proj(i):
        return _dot(h, _wslice(w_in_ref, i * d, d))

    @pl.when(pl.program_id(1) == 0)
    def _():
        abuf_ref[0:CONV_HALO, :] = jnp.zeros((CONV_HALO, d), _F32)

    abuf_ref[CONV_HALO:CONV_HALO + seq_tile, :] = proj(0) * jax.nn.sigmoid(proj(1))
    conv = _causal_depthwise_conv(abuf_ref, shift_ref, convw_ref,
                                  jnp.broadcast_to(convb_ref[...], (seq_tile, d)), seq_tile)
    abuf_ref[0:CONV_HALO, :] = abuf_ref[seq_tile:seq_tile + CONV_HALO, :]
    a = jax.nn.silu(_layernorm(conv, clng_ref[...], clnb_ref[...])).astype(_BF16)
    y_a = _dot(a, _wslice(wco_ref, 0, d))

    u = jax.nn.gelu(proj(2))
    v_ref[...] = _layernorm(jax.nn.gelu(proj(3)), slng_ref[...], slnb_ref[...]).astype(_BF16)
    n_chunks = seq_tile // SGU_CHUNK
    gd = d // SGU_GROUPS
    row = lax.broadcasted_iota(jnp.int32, (SGU_CHUNK, SGU_CHUNK), 0)
    col = lax.broadcasted_iota(jnp.int32, (SGU_CHUNK, SGU_CHUNK), 1)
    causal = col <= row
    for g in range(SGU_GROUPS):
        w_g = jnp.where(causal, sguw_ref[g], 0.0).astype(_BF16)
        v_g = jnp.concatenate(
            [v_ref[n * SGU_CHUNK:(n + 1) * SGU_CHUNK, g * gd:(g + 1) * gd] for n in range(n_chunks)], axis=1)
        z_g = _dot(w_g, v_g) + sgub_ref[:, g:g + 1]
        for n in range(n_chunks):
            sg_ref[n * SGU_CHUNK:(n + 1) * SGU_CHUNK, g * gd:(g + 1) * gd] = z_g[:, n * gd:(n + 1) * gd]
    y_b = _dot((u * sg_ref[...]).astype(_BF16), _wslice(wso_ref, 0, d))

    merged = (jax.nn.sigmoid(proj(4) + bgate_ref[0:1, :]) * y_a
              + jax.nn.sigmoid(proj(5) + bgate_ref[1:2, :]) * y_b)
    o_ref[...] = x + _dot(merged.astype(_BF16), _wslice(wmo_ref, 0, d))


def _xattn_kernel(x_ref, mem_ref, nx_ref, nm_ref, wq_ref, wkv_ref, wxo_ref, o_ref, k_ref, v_ref):
    d = x_ref.shape[-1]
    hd = d // XATTN_HEADS

    @pl.when(pl.program_id(1) == 0)
    def _():
        mem_n = _rmsnorm(mem_ref[...], nm_ref[...]).astype(_BF16)
        k_ref[...] = _dot(mem_n, _wslice(wkv_ref, 0, d)).astype(_BF16)
        v_ref[...] = _dot(mem_n, _wslice(wkv_ref, d, d)).astype(_BF16)

    x = x_ref[...]
    h = _rmsnorm(x, nx_ref[...]).astype(_BF16)
    q = (_dot(h, _wslice(wq_ref, 0, d)) * (hd ** -0.5)).astype(_BF16)
    heads = []
    for i in range(XATTN_HEADS):
        sl = slice(i * hd, (i + 1) * hd)
        s = lax.dot_general(q[:, sl], k_ref[:, sl], (((1,), (1,)), ((), ())), preferred_element_type=_F32)
        e = jnp.exp(s - jnp.max(s, axis=-1, keepdims=True))
        p = e / jnp.sum(e, axis=-1, keepdims=True)
        heads.append(_dot(p.astype(_BF16), v_ref[:, sl]).astype(_BF16))
    o_ref[...] = x + _dot(jnp.concatenate(heads, axis=1), _wslice(wxo_ref, 0, d))


def _ffn_kernel(x_ref, nf_ref, wgu_ref, wdn_ref, nfin_ref, o_ref, *, apply_final_norm):
    d = x_ref.shape[-1]
    f = wdn_ref.shape[1]
    fc = f // FFN_HIDDEN_CHUNKS
    x = x_ref[...]
    h = _rmsnorm(x, nf_ref[...]).astype(_BF16)
    y = x
    for c in range(FFN_HIDDEN_CHUNKS):
        gt = _dot(h, _wslice(wgu_ref, c * fc, fc))
        up = _dot(h, _wslice(wgu_ref, f + c * fc, fc))
        y = y + _dot((jax.nn.silu(gt) * up).astype(_BF16), _wslice(wdn_ref, 0, d, c * fc, fc))
    if apply_final_norm:
        y = _rmsnorm(y, nfin_ref[...])
    o_ref[...] = y


def _resident(shape):
    return pl.BlockSpec(shape, lambda b, s: (0,) * len(shape), pipeline_mode=pl.Buffered(1))


def _seq_tiled_call(name, body, x, seq_tile, resident_args, extra_specs=(), extra_args=(), scratch_shapes=()):
    batch, seq, d = x.shape
    assert seq % seq_tile == 0
    x_spec = pl.BlockSpec((None, seq_tile, d), lambda b, s: (b, s, 0))
    return pl.pallas_call(
        body,
        name=name,
        out_shape=jax.ShapeDtypeStruct(x.shape, x.dtype),
        grid=(batch, seq // seq_tile),
        in_specs=[x_spec, *extra_specs, *[_resident(a.shape) for a in resident_args]],
        out_specs=x_spec,
        scratch_shapes=list(scratch_shapes),
        compiler_params=pltpu.CompilerParams(
            dimension_semantics=("parallel", "arbitrary"), vmem_limit_bytes=VMEM_LIMIT_BYTES),
    )(x, *extra_args, *resident_args)


def _row(v):
    return v.reshape(1, -1)


def kernel(x, mem, norm_mix, w_in, b_gate, conv_w, conv_b, conv_ln_g, conv_ln_b, w_conv_out, sgu_ln_g, sgu_ln_b, sgu_w, sgu_b, w_sgu_out, w_mix_out, norm_xattn, norm_mem, w_q, w_kv, w_xo, norm_ffn, w_gu, w_down, norm_final):
    batch, seq, d = x.shape
    mem_len = mem.shape[1]
    depth = w_in.shape[0]
    assert depth >= 1
    assert d % (SGU_GROUPS * LANES) == 0 and d % (XATTN_HEADS * LANES) == 0
    assert SEQ_TILE_MIXER % SGU_CHUNK == 0 and SEQ_TILE_MIXER >= CONV_HALO
    assert w_down.shape[1] % (FFN_HIDDEN_CHUNKS * LANES) == 0
    lb = _lane_blocked

    for l in range(depth):
        x = _seq_tiled_call(
            "mixer", _mixer_kernel, x, SEQ_TILE_MIXER,
            [_row(norm_mix[l]), lb(w_in[l]), b_gate[l], conv_w[l], _row(conv_b[l]), _row(conv_ln_g[l]),
             _row(conv_ln_b[l]), lb(w_conv_out[l]), _row(sgu_ln_g[l]), _row(sgu_ln_b[l]), sgu_w[l],
             jnp.transpose(sgu_b[l]), lb(w_sgu_out[l]), lb(w_mix_out[l])],
            scratch_shapes=[pltpu.VMEM((CONV_HALO + SEQ_TILE_MIXER, d), _F32),
                            pltpu.VMEM((SUBLANES - 1, CONV_HALO + SEQ_TILE_MIXER, d), _F32),
                            pltpu.VMEM((SEQ_TILE_MIXER, d), _BF16),
                            pltpu.VMEM((SEQ_TILE_MIXER, d), _F32)])
        x = _seq_tiled_call(
            "xattn", _xattn_kernel, x, SEQ_TILE_XATTN,
            [_row(norm_xattn[l]), _row(norm_mem[l]), lb(w_q[l]), lb(w_kv[l]), lb(w_xo[l])],
            extra_specs=[pl.BlockSpec((None, mem_len, d), lambda b, s: (b, 0, 0))], extra_args=[mem],
            scratch_shapes=[pltpu.VMEM((mem_len, d), _BF16), pltpu.VMEM((mem_len, d), _BF16)])
        x = _seq_tiled_call(
            "ffn", functools.partial(_ffn_kernel, apply_final_norm=(l == depth - 1)), x, SEQ_TILE_FFN,
            [_row(norm_ffn[l]), lb(w_gu[l]), lb(w_down[l]), _row(norm_final)])
    return x
```

```python
import functools

import jax
import jax.numpy as jnp
from jax import lax
from jax.experimental import pallas as pl
from jax.experimental.pallas import tpu as pltpu

CONV_WIDTH = 31
SGU_GROUPS = 8
SGU_CHUNK = 128
XATTN_HEADS = 4
RMS_EPS = 1e-6
LN_EPS = 1e-5

SUBLANES = 8
LANES = 128
MXU_TILE = 256
CONV_HALO = -(-(CONV_WIDTH - 1) // SUBLANES) * SUBLANES
VMEM_LIMIT_BYTES = 56 * 1024 * 1024

SEQ_TILE_MIXER = 512
SEQ_TILE_XATTN = 512
SEQ_TILE_FFN = 512
FFN_HIDDEN_CHUNKS = 2

ROW_CHUNK = 2 * SUBLANES
COL_CHUNK = MXU_TILE
CONV_ROWS = 64
CONV_COLS = LANES

_F32 = jnp.float32
_BF16 = jnp.bfloat16


def _dot(a, b):
    return jnp.dot(a, b, preferred_element_type=_F32)


def _lane_blocked(w):
    rows, cols = w.shape
    return jnp.transpose(w.astype(_BF16).reshape(rows, cols // LANES, LANES), (1, 0, 2))


def _wslice(w_ref, col_start, col_size, row_start=0, row_size=None):
    row_size = w_ref.shape[1] if row_size is None else row_size
    blocks = range(col_start // LANES, (col_start + col_size) // LANES)
    return jnp.concatenate([w_ref[j, row_start:row_start + row_size, :] for j in blocks], axis=1)


def _rmsnorm(x, g):
    return x * lax.rsqrt(jnp.mean(x * x, axis=-1, keepdims=True) + RMS_EPS) * g


def _layernorm(x, g, b):
    mu = jnp.mean(x, axis=-1, keepdims=True)
    xc = x - mu
    var = jnp.mean(xc * xc, axis=-1, keepdims=True)
    return xc * lax.rsqrt(var + LN_EPS) * g + b


def _emit_interleaved(mxu_tasks, valu_tasks):
    done = set()
    pending = list(valu_tasks)

    def emit(task):
        task[3]()
        done.add(task[0])

    def drain(budget):
        spent = 0
        progress = True
        while pending and spent < budget and progress:
            progress = False
            for idx, task in enumerate(pending):
                if all(dep in done for dep in task[2]):
                    emit(pending.pop(idx))
                    spent += task[1]
                    progress = True
                    break

    for task in mxu_tasks:
        while not all(dep in done for dep in task[2]):
            before = len(pending)
            drain(1)
            assert len(pending) < before, f"unsatisfiable dependency for {task[0]}"
        emit(task)
        drain(task[1])
    drain(float("inf"))
    assert not pending


def _mixer_kernel(x_ref, nrm_ref, w_in_ref, bgate_ref, convw_ref, convb_ref, clng_ref, clnb_ref,
                  wco_ref, slng_ref, slnb_ref, sguw_ref, sgub_ref, wso_ref, wmo_ref,
                  o_ref, h_ref, abuf_ref, conv_ref, a_ref, p_ref, u_ref, v_ref, sgb_ref, m_ref, mb_ref):
    seq_tile, d = x_ref.shape
    n_rc = seq_tile // ROW_CHUNK
    n_pc = d // COL_CHUNK
    n_rb = seq_tile // CONV_ROWS
    n_cb = d // CONV_COLS
    n_chunks = seq_tile // SGU_CHUNK
    gd = d // SGU_GROUPS

    def rows(c):
        return slice(c * ROW_CHUNK, (c + 1) * ROW_CHUNK)

    def cols(j):
        return slice(j * COL_CHUNK, (j + 1) * COL_CHUNK)

    def t_norm(c):
        h_ref[rows(c), :] = _rmsnorm(x_ref[rows(c), :], nrm_ref[...]).astype(_BF16)

    def t_glu(j):
        av = _dot(h_ref[...], _wslice(w_in_ref, j * COL_CHUNK, COL_CHUNK))
        ag = _dot(h_ref[...], _wslice(w_in_ref, d + j * COL_CHUNK, COL_CHUNK))
        abuf_ref[CONV_HALO:CONV_HALO + seq_tile, cols(j)] = av * jax.nn.sigmoid(ag)

    def t_conv(rb, cb):
        csl = slice(cb * CONV_COLS, (cb + 1) * CONV_COLS)
        r0 = CONV_HALO + rb * CONV_ROWS
        block = abuf_ref[r0 - CONV_HALO:r0 + CONV_ROWS, csl]
        acc = jnp.broadcast_to(convb_ref[:, csl], (CONV_ROWS, CONV_COLS))
        for r in range(SUBLANES):
            win = block if r == 0 else pltpu.roll(block, r, 0)
            for q in range(CONV_HALO // SUBLANES):
                delay = q * SUBLANES + r
                if delay >= CONV_WIDTH:
                    continue
                k = CONV_WIDTH - 1 - delay
                lo = CONV_HALO - q * SUBLANES
                acc = acc + convw_ref[k:k + 1, csl] * win[lo:lo + CONV_ROWS, :]
        conv_ref[rb * CONV_ROWS:(rb + 1) * CONV_ROWS, csl] = acc

    def t_aln(c):
        a_ref[rows(c), :] = jax.nn.silu(_layernorm(conv_ref[rows(c), :], clng_ref[...], clnb_ref[...])).astype(_BF16)

    def t_proj(i, j):
        p_ref[i - 2, :, cols(j)] = _dot(h_ref[...], _wslice(w_in_ref, i * d + j * COL_CHUNK, COL_CHUNK))

    def t_u(c):
        u_ref[rows(c), :] = jax.nn.gelu(p_ref[0, rows(c), :])

    def t_vln(c):
        v_ref[rows(c), :] = _layernorm(jax.nn.gelu(p_ref[1, rows(c), :]), slng_ref[...], slnb_ref[...]).astype(_BF16)

    def t_wco(j):
        g_a = jax.nn.sigmoid(p_ref[2, :, cols(j)] + bgate_ref[0:1, cols(j)])
        m_ref[:, cols(j)] = g_a * _dot(a_ref[...], _wslice(wco_ref, j * COL_CHUNK, COL_CHUNK))

    def t_sgu(g):
        row = lax.broadcasted_iota(jnp.int32, (SGU_CHUNK, SGU_CHUNK), 0)
        col = lax.broadcasted_iota(jnp.int32, (SGU_CHUNK, SGU_CHUNK), 1)
        w_g = jnp.where(col <= row, sguw_ref[g], 0.0).astype(_BF16)
        gsl = slice(g * gd, (g + 1) * gd)
        v_g = jnp.concatenate([v_ref[n * SGU_CHUNK:(n + 1) * SGU_CHUNK, gsl] for n in range(n_chunks)], axis=1)
        z_g = _dot(w_g, v_g) + sgub_ref[:, g:g + 1]
        for n in range(n_chunks):
            rsl = slice(n * SGU_CHUNK, (n + 1) * SGU_CHUNK)
            sgb_ref[rsl, gsl] = (u_ref[rsl, gsl] * z_g[:, n * gd:(n + 1) * gd]).astype(_BF16)

    def t_wso(j):
        y_b = _dot(sgb_ref[...], _wslice(wso_ref, j * COL_CHUNK, COL_CHUNK))
        g_b = jax.nn.sigmoid(p_ref[3, :, cols(j)] + bgate_ref[1:2, cols(j)])
        mb_ref[:, cols(j)] = (m_ref[:, cols(j)] + g_b * y_b).astype(_BF16)

    def t_wmo(j):
        o_ref[:, cols(j)] = x_ref[:, cols(j)] + _dot(mb_ref[...], _wslice(wmo_ref, j * COL_CHUNK, COL_CHUNK))

    @pl.when(pl.program_id(1) == 0)
    def _():
        abuf_ref[0:CONV_HALO, :] = jnp.zeros((CONV_HALO, d), _F32)

    task = functools.partial
    all_norm = [f"norm{c}" for c in range(n_rc)]
    all_aln = [f"aln{c}" for c in range(n_rc)]
    all_vln = [f"vln{c}" for c in range(n_rc)]
    all_u = [f"u{c}" for c in range(n_rc)]
    all_sgu = [f"sgu{g}" for g in range(SGU_GROUPS)]
    all_wso = [f"wso{j}" for j in range(n_pc)]
    mxu = [(f"glu{j}", 512, all_norm, task(t_glu, j)) for j in range(n_pc)]
    for i in (3, 2, 4, 5):
        mxu += [(f"proj{i}_{j}", 256, all_norm, task(t_proj, i, j)) for j in range(n_pc)]
    mxu += [(f"wco{j}", 256, all_aln + [f"proj4_{j}"], task(t_wco, j)) for j in range(n_pc)]
    mxu += [(f"sgu{g}", 64, all_vln + all_u, task(t_sgu, g)) for g in range(SGU_GROUPS)]
    mxu += [(f"wso{j}", 256, all_sgu + [f"wco{j}", f"proj5_{j}"], task(t_wso, j)) for j in range(n_pc)]
    mxu += [(f"wmo{j}", 256, all_wso, task(t_wmo, j)) for j in range(n_pc)]

    valu = [(f"norm{c}", 35, [], task(t_norm, c)) for c in range(n_rc)]
    for cb in range(n_cb):
        for rb in range(n_rb):
            valu.append((f"conv{rb}_{cb}", 170, [f"glu{cb * CONV_COLS // COL_CHUNK}"], task(t_conv, rb, cb)))
    rows_per_block = CONV_ROWS // ROW_CHUNK
    valu += [(f"aln{c}", 60, [f"conv{c // rows_per_block}_{cb}" for cb in range(n_cb)], task(t_aln, c))
             for c in range(n_rc)]
    valu += [(f"vln{c}", 80, [f"proj3_{j}" for j in range(n_pc)], task(t_vln, c)) for c in range(n_rc)]
    valu += [(f"u{c}", 40, [f"proj2_{j}" for j in range(n_pc)], task(t_u, c)) for c in range(n_rc)]
    _emit_interleaved(mxu, valu)
    abuf_ref[0:CONV_HALO, :] = abuf_ref[seq_tile:seq_tile + CONV_HALO, :]


def _xattn_kernel(x_ref, mem_ref, nx_ref, nm_ref, wq_ref, wkv_ref, wxo_ref, o_ref, k_ref, v_ref):
    d = x_ref.shape[-1]
    hd = d // XATTN_HEADS

    @pl.when(pl.program_id(1) == 0)
    def _():
        mem_n = _rmsnorm(mem_ref[...], nm_ref[...]).astype(_BF16)
        k_ref[...] = _dot(mem_n, _wslice(wkv_ref, 0, d)).astype(_BF16)
        v_ref[...] = _dot(mem_n, _wslice(wkv_ref, d, d)).astype(_BF16)

    x = x_ref[...]
    h = _rmsnorm(x, nx_ref[...]).astype(_BF16)
    q = (_dot(h, _wslice(wq_ref, 0, d)) * (hd ** -0.5)).astype(_BF16)
    heads = []
    for i in range(XATTN_HEADS):
        sl = slice(i * hd, (i + 1) * hd)
        s = lax.dot_general(q[:, sl], k_ref[:, sl], (((1,), (1,)), ((), ())), preferred_element_type=_F32)
        e = jnp.exp(s - jnp.max(s, axis=-1, keepdims=True))
        p = e / jnp.sum(e, axis=-1, keepdims=True)
        heads.append(_dot(p.astype(_BF16), v_ref[:, sl]).astype(_BF16))
    o_ref[...] = x + _dot(jnp.concatenate(heads, axis=1), _wslice(wxo_ref, 0, d))


def _hidden_chunks(f):
    tiles = f // MXU_TILE
    bounds = [MXU_TILE * (tiles * c // FFN_HIDDEN_CHUNKS) for c in range(FFN_HIDDEN_CHUNKS)] + [f]
    return list(zip(bounds[:-1], bounds[1:]))


def _ffn_kernel(x_ref, nf_ref, wgu_ref, wdn_ref, nfin_ref, o_ref, *, apply_final_norm):
    d = x_ref.shape[-1]
    f = wdn_ref.shape[1]
    x = x_ref[...]
    h = _rmsnorm(x, nf_ref[...]).astype(_BF16)
    y = x
    for lo, hi in _hidden_chunks(f):
        gt = _dot(h, _wslice(wgu_ref, lo, hi - lo))
        up = _dot(h, _wslice(wgu_ref, f + lo, hi - lo))
        y = y + _dot((jax.nn.silu(gt) * up).astype(_BF16), _wslice(wdn_ref, 0, d, lo, hi - lo))
    if apply_final_norm:
        y = _rmsnorm(y, nfin_ref[...])
    o_ref[...] = y


def _resident(shape):
    return pl.BlockSpec(shape, lambda b, s: (0,) * len(shape), pipeline_mode=pl.Buffered(1))


def _seq_tiled_call(name, body, x, seq_tile, resident_args, extra_specs=(), extra_args=(), scratch_shapes=()):
    batch, seq, d = x.shape
    assert seq % seq_tile == 0
    x_spec = pl.BlockSpec((None, seq_tile, d), lambda b, s: (b, s, 0))
    return pl.pallas_call(
        body,
        name=name,
        out_shape=jax.ShapeDtypeStruct(x.shape, x.dtype),
        grid=(batch, seq // seq_tile),
        in_specs=[x_spec, *extra_specs, *[_resident(a.shape) for a in resident_args]],
        out_specs=x_spec,
        scratch_shapes=list(scratch_shapes),
        compiler_params=pltpu.CompilerParams(
            dimension_semantics=("parallel", "arbitrary"), vmem_limit_bytes=VMEM_LIMIT_BYTES),
    )(x, *extra_args, *resident_args)


def _row(v):
    return v.reshape(1, -1)


def kernel(x, mem, norm_mix, w_in, b_gate, conv_w, conv_b, conv_ln_g, conv_ln_b, w_conv_out, sgu_ln_g, sgu_ln_b, sgu_w, sgu_b, w_sgu_out, w_mix_out, norm_xattn, norm_mem, w_q, w_kv, w_xo, norm_ffn, w_gu, w_down, norm_final):
    batch, seq, d = x.shape
    mem_len = mem.shape[1]
    depth = w_in.shape[0]
    tm = SEQ_TILE_MIXER
    assert depth >= 1
    assert d % (SGU_GROUPS * LANES) == 0 and d % (XATTN_HEADS * LANES) == 0 and d % COL_CHUNK == 0
    assert tm % SGU_CHUNK == 0 and tm % CONV_ROWS == 0 and CONV_ROWS % ROW_CHUNK == 0
    assert w_down.shape[1] % LANES == 0
    lb = _lane_blocked

    for l in range(depth):
        x = _seq_tiled_call(
            "mixer", _mixer_kernel, x, tm,
            [_row(norm_mix[l]), lb(w_in[l]), b_gate[l], conv_w[l], _row(conv_b[l]), _row(conv_ln_g[l]),
             _row(conv_ln_b[l]), lb(w_conv_out[l]), _row(sgu_ln_g[l]), _row(sgu_ln_b[l]), sgu_w[l],
             jnp.transpose(sgu_b[l]), lb(w_sgu_out[l]), lb(w_mix_out[l])],
            scratch_shapes=[pltpu.VMEM((tm, d), _BF16),
                            pltpu.VMEM((CONV_HALO + tm, d), _F32),
                            pltpu.VMEM((tm, d), _F32),
                            pltpu.VMEM((tm, d), _BF16),
                            pltpu.VMEM((4, tm, d), _F32),
                            pltpu.VMEM((tm, d), _F32),
                            pltpu.VMEM((tm, d), _BF16),
                            pltpu.VMEM((tm, d), _BF16),
                            pltpu.VMEM((tm, d), _F32),
                            pltpu.VMEM((tm, d), _BF16)])
        x = _seq_tiled_call(
            "xattn", _xattn_kernel, x, SEQ_TILE_XATTN,
            [_row(norm_xattn[l]), _row(norm_mem[l]), lb(w_q[l]), lb(w_kv[l]), lb(w_xo[l])],
            extra_specs=[pl.BlockSpec((None, mem_len, d), lambda b, s: (b, 0, 0))], extra_args=[mem],
            scratch_shapes=[pltpu.VMEM((mem_len, d), _BF16), pltpu.VMEM((mem_len, d), _BF16)])
        x = _seq_tiled_call(
            "ffn", functools.partial(_ffn_kernel, apply_final_norm=(l == depth - 1)), x, SEQ_TILE_FFN,
            [_row(norm_ffn[l]), lb(w_gu[l]), lb(w_down[l]), _row(norm_final)])
    return x
```

```python
import functools

import jax
import jax.numpy as jnp
from jax import lax
from jax.experimental import pallas as pl
from jax.experimental.pallas import tpu as pltpu

CONV_WIDTH = 31
SGU_GROUPS = 8
SGU_CHUNK = 128
XATTN_HEADS = 4
RMS_EPS = 1e-6
LN_EPS = 1e-5

SUBLANES = 8
LANES = 128
MXU_TILE = 256
CONV_HALO = -(-(CONV_WIDTH - 1) // SUBLANES) * SUBLANES
VMEM_LIMIT_BYTES = 56 * 1024 * 1024

SEQ_TILE_MIXER = 512
SEQ_TILE_XATTN = 512
SEQ_TILE_FFN = 512
FFN_HIDDEN_CHUNKS = 2

ROW_CHUNK = 2 * SUBLANES
COL_CHUNK = MXU_TILE
CONV_ROWS = 64
CONV_COLS = LANES

_F32 = jnp.float32
_BF16 = jnp.bfloat16


def _dot(a, b):
    return jnp.dot(a, b, preferred_element_type=_F32)


def _lane_blocked(w):
    rows, cols = w.shape
    return jnp.transpose(w.astype(_BF16).reshape(rows, cols // LANES, LANES), (1, 0, 2))


def _wslice(w_ref, col_start, col_size, row_start=0, row_size=None):
    row_size = w_ref.shape[1] if row_size is None else row_size
    blocks = range(col_start // LANES, (col_start + col_size) // LANES)
    return jnp.concatenate([w_ref[j, row_start:row_start + row_size, :] for j in blocks], axis=1)


def _rmsnorm(x, g):
    return x * lax.rsqrt(jnp.mean(x * x, axis=-1, keepdims=True) + RMS_EPS) * g


def _layernorm(x, g, b):
    mu = jnp.mean(x, axis=-1, keepdims=True)
    xc = x - mu
    var = jnp.mean(xc * xc, axis=-1, keepdims=True)
    return xc * lax.rsqrt(var + LN_EPS) * g + b


def _delay_rows(block, r):
    n = block.shape[0] // SUBLANES
    tiles = [pltpu.roll(block[i * SUBLANES:(i + 1) * SUBLANES, :], r, 0) for i in range(n)]
    low = lax.broadcasted_iota(jnp.int32, tiles[0].shape, 0) < r
    return jnp.concatenate([tiles[0]] + [jnp.where(low, tiles[i - 1], tiles[i]) for i in range(1, n)], axis=0)


def _emit_interleaved(mxu_tasks, valu_tasks):
    done = set()
    pending = list(valu_tasks)

    def emit(task):
        task[3]()
        done.add(task[0])

    def drain(budget):
        spent = 0
        progress = True
        while pending and spent < budget and progress:
            progress = False
            for idx, task in enumerate(pending):
                if all(dep in done for dep in task[2]):
                    emit(pending.pop(idx))
                    spent += task[1]
                    progress = True
                    break

    for task in mxu_tasks:
        while not all(dep in done for dep in task[2]):
            before = len(pending)
            drain(1)
            assert len(pending) < before, f"unsatisfiable dependency for {task[0]}"
        emit(task)
        drain(task[1])
    drain(float("inf"))
    assert not pending


def _mixer_kernel(x_ref, nrm_ref, w_in_ref, bgate_ref, convw_ref, convb_ref, clng_ref, clnb_ref,
                  wco_ref, slng_ref, slnb_ref, sguw_ref, sgub_ref, wso_ref, wmo_ref,
                  o_ref, h_ref, abuf_ref, conv_ref, a_ref, p_ref, u_ref, v_ref, sgb_ref, m_ref, mb_ref):
    seq_tile, d = x_ref.shape
    n_rc = seq_tile // ROW_CHUNK
    n_pc = d // COL_CHUNK
    n_rb = seq_tile // CONV_ROWS
    n_cb = d // CONV_COLS
    n_chunks = seq_tile // SGU_CHUNK
    gd = d // SGU_GROUPS
    lanes_per_piece = COL_CHUNK // LANES

    def rows(c):
        return slice(c * ROW_CHUNK, (c + 1) * ROW_CHUNK)

    def cols(j):
        return slice(j * COL_CHUNK, (j + 1) * COL_CHUNK)

    def p_rows(i, c):
        return jnp.concatenate([p_ref[i * n_pc + j, rows(c), :] for j in range(n_pc)], axis=1)

    def t_norm(c):
        h_ref[rows(c), :] = _rmsnorm(x_ref[rows(c), :], nrm_ref[...]).astype(_BF16)

    def t_glu(j):
        av = _dot(h_ref[...], _wslice(w_in_ref, j * COL_CHUNK, COL_CHUNK))
        ag = _dot(h_ref[...], _wslice(w_in_ref, d + j * COL_CHUNK, COL_CHUNK))
        a = av * jax.nn.sigmoid(ag)
        for b in range(lanes_per_piece):
            abuf_ref[j * lanes_per_piece + b, CONV_HALO:CONV_HALO + seq_tile, :] = a[:, b * LANES:(b + 1) * LANES]

    def conv_block(cb, row0):
        block = abuf_ref[cb, pl.ds(row0, CONV_HALO + CONV_ROWS), :]
        acc = jnp.broadcast_to(convb_ref[cb], (CONV_ROWS, CONV_COLS))
        for r in range(SUBLANES):
            win = block if r == 0 else _delay_rows(block, r)
            for q in range(CONV_HALO // SUBLANES):
                delay = q * SUBLANES + r
                if delay >= CONV_WIDTH:
                    continue
                k = CONV_WIDTH - 1 - delay
                lo = CONV_HALO - q * SUBLANES
                acc = acc + convw_ref[cb, k:k + 1, :] * win[lo:lo + CONV_ROWS, :]
        conv_ref[cb, pl.ds(row0, CONV_ROWS), :] = acc

    def t_aln(c):
        conv = jnp.concatenate([conv_ref[cb, rows(c), :] for cb in range(n_cb)], axis=1)
        a_ref[rows(c), :] = jax.nn.silu(_layernorm(conv, clng_ref[...], clnb_ref[...])).astype(_BF16)

    def t_u(c):
        u_ref[rows(c), :] = jax.nn.gelu(p_rows(0, c))

    def t_vln(c):
        v_ref[rows(c), :] = _layernorm(jax.nn.gelu(p_rows(1, c)), slng_ref[...], slnb_ref[...]).astype(_BF16)

    def t_wco(j):
        g_a = jax.nn.sigmoid(p_ref[2 * n_pc + j] + bgate_ref[0:1, cols(j)])
        m_ref[:, cols(j)] = g_a * _dot(a_ref[...], _wslice(wco_ref, j * COL_CHUNK, COL_CHUNK))

    def t_sgu(g):
        row = lax.broadcasted_iota(jnp.int32, (SGU_CHUNK, SGU_CHUNK), 0)
        col = lax.broadcasted_iota(jnp.int32, (SGU_CHUNK, SGU_CHUNK), 1)
        w_g = jnp.where(col <= row, sguw_ref[g], 0.0).astype(_BF16)
        gsl = slice(g * gd, (g + 1) * gd)
        v_g = jnp.concatenate([v_ref[n * SGU_CHUNK:(n + 1) * SGU_CHUNK, gsl] for n in range(n_chunks)], axis=1)
        z_g = _dot(w_g, v_g) + sgub_ref[:, g:g + 1]
        for n in range(n_chunks):
            rsl = slice(n * SGU_CHUNK, (n + 1) * SGU_CHUNK)
            sgb_ref[rsl, gsl] = (u_ref[rsl, gsl] * z_g[:, n * gd:(n + 1) * gd]).astype(_BF16)

    def t_wso(j):
        y_b = _dot(sgb_ref[...], _wslice(wso_ref, j * COL_CHUNK, COL_CHUNK))
        g_b = jax.nn.sigmoid(p_ref[3 * n_pc + j] + bgate_ref[1:2, cols(j)])
        mb_ref[:, cols(j)] = (m_ref[:, cols(j)] + g_b * y_b).astype(_BF16)

    def t_wmo(j):
        o_ref[:, cols(j)] = x_ref[:, cols(j)] + _dot(mb_ref[...], _wslice(wmo_ref, j * COL_CHUNK, COL_CHUNK))

    @pl.when(pl.program_id(1) == 0)
    def _():
        abuf_ref[:, 0:CONV_HALO, :] = jnp.zeros((n_cb, CONV_HALO, CONV_COLS), _F32)

    for c in range(n_rc):
        t_norm(c)
    for j in range(n_pc):
        t_glu(j)

    n_pieces = 4 * n_pc
    conv_per_iter = n_cb * n_rb // n_pieces
    iters_per_cb = n_rb // conv_per_iter

    def paired(t, carry):
        blk = (2 * n_pc + t) * lanes_per_piece
        p_ref[t] = _dot(h_ref[...], jnp.concatenate([w_in_ref[blk + b] for b in range(lanes_per_piece)], axis=1))
        cb = t // iters_per_cb
        rb0 = (t % iters_per_cb) * conv_per_iter
        for i in range(conv_per_iter):
            conv_block(cb, pl.multiple_of((rb0 + i) * CONV_ROWS, CONV_ROWS))
        return carry

    lax.fori_loop(0, n_pieces, paired, 0)
    abuf_ref[:, 0:CONV_HALO, :] = abuf_ref[:, seq_tile:seq_tile + CONV_HALO, :]

    task = functools.partial
    all_aln = [f"aln{c}" for c in range(n_rc)]
    all_vln = [f"vln{c}" for c in range(n_rc)]
    all_u = [f"u{c}" for c in range(n_rc)]
    all_sgu = [f"sgu{g}" for g in range(SGU_GROUPS)]
    all_wso = [f"wso{j}" for j in range(n_pc)]
    mxu = [(f"wco{j}", 256, all_aln, task(t_wco, j)) for j in range(n_pc)]
    mxu += [(f"sgu{g}", 64, all_vln + all_u, task(t_sgu, g)) for g in range(SGU_GROUPS)]
    mxu += [(f"wso{j}", 256, all_sgu + [f"wco{j}"], task(t_wso, j)) for j in range(n_pc)]
    mxu += [(f"wmo{j}", 256, all_wso, task(t_wmo, j)) for j in range(n_pc)]
    valu = [(f"aln{c}", 60, [], task(t_aln, c)) for c in range(n_rc)]
    valu += [(f"vln{c}", 80, [], task(t_vln, c)) for c in range(n_rc)]
    valu += [(f"u{c}", 40, [], task(t_u, c)) for c in range(n_rc)]
    _emit_interleaved(mxu, valu)


def _xattn_kernel(x_ref, mem_ref, nx_ref, nm_ref, wq_ref, wkv_ref, wxo_ref, o_ref, k_ref, v_ref):
    d = x_ref.shape[-1]
    hd = d // XATTN_HEADS

    @pl.when(pl.program_id(1) == 0)
    def _():
        mem_n = _rmsnorm(mem_ref[...], nm_ref[...]).astype(_BF16)
        k_ref[...] = _dot(mem_n, _wslice(wkv_ref, 0, d)).astype(_BF16)
        v_ref[...] = _dot(mem_n, _wslice(wkv_ref, d, d)).astype(_BF16)

    x = x_ref[...]
    h = _rmsnorm(x, nx_ref[...]).astype(_BF16)
    q = (_dot(h, _wslice(wq_ref, 0, d)) * (hd ** -0.5)).astype(_BF16)
    heads = []
    for i in range(XATTN_HEADS):
        sl = slice(i * hd, (i + 1) * hd)
        s = lax.dot_general(q[:, sl], k_ref[:, sl], (((1,), (1,)), ((), ())), preferred_element_type=_F32)
        e = jnp.exp(s - jnp.max(s, axis=-1, keepdims=True))
        p = e / jnp.sum(e, axis=-1, keepdims=True)
        heads.append(_dot(p.astype(_BF16), v_ref[:, sl]).astype(_BF16))
    o_ref[...] = x + _dot(jnp.concatenate(heads, axis=1), _wslice(wxo_ref, 0, d))


def _hidden_chunks(f):
    tiles = f // MXU_TILE
    bounds = [MXU_TILE * (tiles * c // FFN_HIDDEN_CHUNKS) for c in range(FFN_HIDDEN_CHUNKS)] + [f]
    return list(zip(bounds[:-1], bounds[1:]))


def _ffn_kernel(x_ref, nf_ref, wgu_ref, wdn_ref, nfin_ref, o_ref, *, apply_final_norm):
    d = x_ref.shape[-1]
    f = wdn_ref.shape[1]
    x = x_ref[...]
    h = _rmsnorm(x, nf_ref[...]).astype(_BF16)
    y = x
    for lo, hi in _hidden_chunks(f):
        gt = _dot(h, _wslice(wgu_ref, lo, hi - lo))
        up = _dot(h, _wslice(wgu_ref, f + lo, hi - lo))
        y = y + _dot((jax.nn.silu(gt) * up).astype(_BF16), _wslice(wdn_ref, 0, d, lo, hi - lo))
    if apply_final_norm:
        y = _rmsnorm(y, nfin_ref[...])
    o_ref[...] = y


def _resident(shape):
    return pl.BlockSpec(shape, lambda b, s: (0,) * len(shape), pipeline_mode=pl.Buffered(1))


def _seq_tiled_call(name, body, x, seq_tile, resident_args, extra_specs=(), extra_args=(), scratch_shapes=()):
    batch, seq, d = x.shape
    assert seq % seq_tile == 0
    x_spec = pl.BlockSpec((None, seq_tile, d), lambda b, s: (b, s, 0))
    return pl.pallas_call(
        body,
        name=name,
        out_shape=jax.ShapeDtypeStruct(x.shape, x.dtype),
        grid=(batch, seq // seq_tile),
        in_specs=[x_spec, *extra_specs, *[_resident(a.shape) for a in resident_args]],
        out_specs=x_spec,
        scratch_shapes=list(scratch_shapes),
        compiler_params=pltpu.CompilerParams(
            dimension_semantics=("parallel", "arbitrary"), vmem_limit_bytes=VMEM_LIMIT_BYTES),
    )(x, *extra_args, *resident_args)


def _row(v):
    return v.reshape(1, -1)


def kernel(x, mem, norm_mix, w_in, b_gate, conv_w, conv_b, conv_ln_g, conv_ln_b, w_conv_out, sgu_ln_g, sgu_ln_b, sgu_w, sgu_b, w_sgu_out, w_mix_out, norm_xattn, norm_mem, w_q, w_kv, w_xo, norm_ffn, w_gu, w_down, norm_final):
    batch, seq, d = x.shape
    mem_len = mem.shape[1]
    depth = w_in.shape[0]
    tm = SEQ_TILE_MIXER
    n_cb = d // CONV_COLS
    n_pieces = 4 * d // COL_CHUNK
    assert depth >= 1
    assert d % (SGU_GROUPS * LANES) == 0 and d % (XATTN_HEADS * LANES) == 0 and d % COL_CHUNK == 0
    assert tm % SGU_CHUNK == 0 and tm % CONV_ROWS == 0 and tm % ROW_CHUNK == 0
    assert (n_cb * (tm // CONV_ROWS)) % n_pieces == 0 and n_pieces % n_cb == 0
    assert w_down.shape[1] % LANES == 0
    lb = _lane_blocked

    for l in range(depth):
        x = _seq_tiled_call(
            "mixer", _mixer_kernel, x, tm,
            [_row(norm_mix[l]), lb(w_in[l]), b_gate[l],
             jnp.transpose(conv_w[l].reshape(CONV_WIDTH, n_cb, CONV_COLS), (1, 0, 2)),
             conv_b[l].reshape(n_cb, 1, CONV_COLS), _row(conv_ln_g[l]),
             _row(conv_ln_b[l]), lb(w_conv_out[l]), _row(sgu_ln_g[l]), _row(sgu_ln_b[l]), sgu_w[l],
             jnp.transpose(sgu_b[l]), lb(w_sgu_out[l]), lb(w_mix_out[l])],
            scratch_shapes=[pltpu.VMEM((tm, d), _BF16),
                            pltpu.VMEM((n_cb, CONV_HALO + tm, CONV_COLS), _F32),
                            pltpu.VMEM((n_cb, tm, CONV_COLS), _F32),
                            pltpu.VMEM((tm, d), _BF16),
                            pltpu.VMEM((n_pieces, tm, COL_CHUNK), _F32),
                            pltpu.VMEM((tm, d), _F32),
                            pltpu.VMEM((tm, d), _BF16),
                            pltpu.VMEM((tm, d), _BF16),
                            pltpu.VMEM((tm, d), _F32),
                            pltpu.VMEM((tm, d), _BF16)])
        x = _seq_tiled_call(
            "xattn", _xattn_kernel, x, SEQ_TILE_XATTN,
            [_row(norm_xattn[l]), _row(norm_mem[l]), lb(w_q[l]), lb(w_kv[l]), lb(w_xo[l])],
            extra_specs=[pl.BlockSpec((None, mem_len, d), lambda b, s: (b, 0, 0))], extra_args=[mem],
            scratch_shapes=[pltpu.VMEM((mem_len, d), _BF16), pltpu.VMEM((mem_len, d), _BF16)])
        x = _seq_tiled_call(
            "ffn", functools.partial(_ffn_kernel, apply_final_norm=(l == depth - 1)), x, SEQ_TILE_FFN,
            [_row(norm_ffn[l]), lb(w_gu[l]), lb(w_down[l]), _row(norm_final)])
    return x
```

```python
import functools

import jax
import jax.numpy as jnp
from jax import lax
from jax.experimental import pallas as pl
from jax.experimental.pallas import tpu as pltpu

CONV_WIDTH = 31
SGU_GROUPS = 8
SGU_CHUNK = 128
XATTN_HEADS = 4
RMS_EPS = 1e-6
LN_EPS = 1e-5

SUBLANES = 8
LANES = 128
MXU_TILE = 256
CONV_HALO = -(-(CONV_WIDTH - 1) // SUBLANES) * SUBLANES
VMEM_LIMIT_BYTES = 56 * 1024 * 1024

SEQ_TILE_MIXER = 512
SEQ_TILE_XATTN = 1024
SEQ_TILE_FFN = 1024
FFN_HIDDEN_CHUNKS = 2
FFN_SUB_TILES = 4

ROW_CHUNK = 2 * SUBLANES
COL_CHUNK = MXU_TILE
CONV_ROWS = 64
CONV_COLS = LANES

_F32 = jnp.float32
_BF16 = jnp.bfloat16


def _dot(a, b):
    return jnp.dot(a, b, preferred_element_type=_F32)


def _lane_blocked(w):
    rows, cols = w.shape
    return jnp.transpose(w.astype(_BF16).reshape(rows, cols // LANES, LANES), (1, 0, 2))


def _wslice(w_ref, col_start, col_size, row_start=0, row_size=None):
    row_size = w_ref.shape[1] if row_size is None else row_size
    blocks = range(col_start // LANES, (col_start + col_size) // LANES)
    return jnp.concatenate([w_ref[j, row_start:row_start + row_size, :] for j in blocks], axis=1)


def _rmsnorm(x, g):
    return x * lax.rsqrt(jnp.mean(x * x, axis=-1, keepdims=True) + RMS_EPS) * g


def _layernorm(x, g, b):
    mu = jnp.mean(x, axis=-1, keepdims=True)
    xc = x - mu
    var = jnp.mean(xc * xc, axis=-1, keepdims=True)
    return xc * lax.rsqrt(var + LN_EPS) * g + b


def _emit_interleaved(mxu_tasks, valu_tasks):
    done = set()
    pending = list(valu_tasks)

    def emit(task):
        task[3]()
        done.add(task[0])

    def drain(budget):
        spent = 0
        progress = True
        while pending and spent < budget and progress:
            progress = False
            for idx, task in enumerate(pending):
                if all(dep in done for dep in task[2]):
                    emit(pending.pop(idx))
                    spent += task[1]
                    progress = True
                    break

    for task in mxu_tasks:
        while not all(dep in done for dep in task[2]):
            before = len(pending)
            drain(1)
            assert len(pending) < before, f"unsatisfiable dependency for {task[0]}"
        emit(task)
        drain(task[1])
    drain(float("inf"))
    assert not pending


def _mixer_kernel(x_ref, nrm_ref, w_in_ref, bgate_ref, convw_ref, convb_ref, clng_ref, clnb_ref,
                  wco_ref, slng_ref, slnb_ref, sguw_ref, sgub_ref, wso_ref, wmo_ref,
                  o_ref, h_ref, abuf_ref, conv_ref, a_ref, p_ref, u_ref, v_ref, sgb_ref, m_ref, mb_ref):
    seq_tile, d = x_ref.shape
    n_rc = seq_tile // ROW_CHUNK
    n_pc = d // COL_CHUNK
    n_rb = seq_tile // CONV_ROWS
    n_cb = d // CONV_COLS
    n_chunks = seq_tile // SGU_CHUNK
    gd = d // SGU_GROUPS

    def rows(c):
        return slice(c * ROW_CHUNK, (c + 1) * ROW_CHUNK)

    def cols(j):
        return slice(j * COL_CHUNK, (j + 1) * COL_CHUNK)

    def t_norm(c):
        h_ref[rows(c), :] = _rmsnorm(x_ref[rows(c), :], nrm_ref[...]).astype(_BF16)

    def t_glu(j):
        av = _dot(h_ref[...], _wslice(w_in_ref, j * COL_CHUNK, COL_CHUNK))
        ag = _dot(h_ref[...], _wslice(w_in_ref, d + j * COL_CHUNK, COL_CHUNK))
        abuf_ref[CONV_HALO:CONV_HALO + seq_tile, cols(j)] = av * jax.nn.sigmoid(ag)

    def t_conv(rb, cb):
        csl = slice(cb * CONV_COLS, (cb + 1) * CONV_COLS)
        r0 = CONV_HALO + rb * CONV_ROWS
        block = abuf_ref[r0 - CONV_HALO:r0 + CONV_ROWS, csl]
        acc = jnp.broadcast_to(convb_ref[:, csl], (CONV_ROWS, CONV_COLS))
        for r in range(SUBLANES):
            win = block if r == 0 else pltpu.roll(block, r, 0)
            for q in range(CONV_HALO // SUBLANES):
                delay = q * SUBLANES + r
                if delay >= CONV_WIDTH:
                    continue
                k = CONV_WIDTH - 1 - delay
                lo = CONV_HALO - q * SUBLANES
                acc = acc + convw_ref[k:k + 1, csl] * win[lo:lo + CONV_ROWS, :]
        conv_ref[rb * CONV_ROWS:(rb + 1) * CONV_ROWS, csl] = acc

    def t_aln(c):
        a_ref[rows(c), :] = jax.nn.silu(_layernorm(conv_ref[rows(c), :], clng_ref[...], clnb_ref[...])).astype(_BF16)

    def t_proj(i, j):
        p_ref[i - 2, :, cols(j)] = _dot(h_ref[...], _wslice(w_in_ref, i * d + j * COL_CHUNK, COL_CHUNK))

    def t_u(c):
        u_ref[rows(c), :] = jax.nn.gelu(p_ref[0, rows(c), :])

    def t_vln(c):
        v_ref[rows(c), :] = _layernorm(jax.nn.gelu(p_ref[1, rows(c), :]), slng_ref[...], slnb_ref[...]).astype(_BF16)

    def t_wco(j):
        g_a = jax.nn.sigmoid(p_ref[2, :, cols(j)] + bgate_ref[0:1, cols(j)])
        m_ref[:, cols(j)] = g_a * _dot(a_ref[...], _wslice(wco_ref, j * COL_CHUNK, COL_CHUNK))

    def t_sgu(g):
        row = lax.broadcasted_iota(jnp.int32, (SGU_CHUNK, SGU_CHUNK), 0)
        col = lax.broadcasted_iota(jnp.int32, (SGU_CHUNK, SGU_CHUNK), 1)
        w_g = jnp.where(col <= row, sguw_ref[g], 0.0).astype(_BF16)
        gsl = slice(g * gd, (g + 1) * gd)
        v_g = jnp.concatenate([v_ref[n * SGU_CHUNK:(n + 1) * SGU_CHUNK, gsl] for n in range(n_chunks)], axis=1)
        z_g = _dot(w_g, v_g) + sgub_ref[:, g:g + 1]
        for n in range(n_chunks):
            rsl = slice(n * SGU_CHUNK, (n + 1) * SGU_CHUNK)
            sgb_ref[rsl, gsl] = (u_ref[rsl, gsl] * z_g[:, n * gd:(n + 1) * gd]).astype(_BF16)

    def t_wso(j):
        y_b = _dot(sgb_ref[...], _wslice(wso_ref, j * COL_CHUNK, COL_CHUNK))
        g_b = jax.nn.sigmoid(p_ref[3, :, cols(j)] + bgate_ref[1:2, cols(j)])
        mb_ref[:, cols(j)] = (m_ref[:, cols(j)] + g_b * y_b).astype(_BF16)

    def t_wmo(j):
        o_ref[:, cols(j)] = x_ref[:, cols(j)] + _dot(mb_ref[...], _wslice(wmo_ref, j * COL_CHUNK, COL_CHUNK))

    @pl.when(pl.program_id(1) == 0)
    def _():
        abuf_ref[0:CONV_HALO, :] = jnp.zeros((CONV_HALO, d), _F32)

    task = functools.partial
    all_norm = [f"norm{c}" for c in range(n_rc)]
    all_aln = [f"aln{c}" for c in range(n_rc)]
    all_vln = [f"vln{c}" for c in range(n_rc)]
    all_u = [f"u{c}" for c in range(n_rc)]
    all_sgu = [f"sgu{g}" for g in range(SGU_GROUPS)]
    all_wso = [f"wso{j}" for j in range(n_pc)]
    mxu = [(f"glu{j}", 512, all_norm, task(t_glu, j)) for j in range(n_pc)]
    for i in (3, 2, 4, 5):
        mxu += [(f"proj{i}_{j}", 256, all_norm, task(t_proj, i, j)) for j in range(n_pc)]
    mxu += [(f"wco{j}", 256, all_aln + [f"proj4_{j}"], task(t_wco, j)) for j in range(n_pc)]
    mxu += [(f"sgu{g}", 64, all_vln + all_u, task(t_sgu, g)) for g in range(SGU_GROUPS)]
    mxu += [(f"wso{j}", 256, all_sgu + [f"wco{j}", f"proj5_{j}"], task(t_wso, j)) for j in range(n_pc)]
    mxu += [(f"wmo{j}", 256, all_wso, task(t_wmo, j)) for j in range(n_pc)]

    valu = [(f"norm{c}", 35, [], task(t_norm, c)) for c in range(n_rc)]
    for cb in range(n_cb):
        for rb in range(n_rb):
            valu.append((f"conv{rb}_{cb}", 170, [f"glu{cb * CONV_COLS // COL_CHUNK}"], task(t_conv, rb, cb)))
    rows_per_block = CONV_ROWS // ROW_CHUNK
    valu += [(f"aln{c}", 60, [f"conv{c // rows_per_block}_{cb}" for cb in range(n_cb)], task(t_aln, c))
             for c in range(n_rc)]
    valu += [(f"vln{c}", 80, [f"proj3_{j}" for j in range(n_pc)], task(t_vln, c)) for c in range(n_rc)]
    valu += [(f"u{c}", 40, [f"proj2_{j}" for j in range(n_pc)], task(t_u, c)) for c in range(n_rc)]
    _emit_interleaved(mxu, valu)
    abuf_ref[0:CONV_HALO, :] = abuf_ref[seq_tile:seq_tile + CONV_HALO, :]


def _xattn_kernel(x_ref, mem_ref, nx_ref, nm_ref, wq_ref, wkv_ref, wxo_ref, o_ref, k_ref, v_ref):
    d = x_ref.shape[-1]
    hd = d // XATTN_HEADS

    @pl.when(pl.program_id(1) == 0)
    def _():
        mem_n = _rmsnorm(mem_ref[...], nm_ref[...]).astype(_BF16)
        k_ref[...] = _dot(mem_n, _wslice(wkv_ref, 0, d)).astype(_BF16)
        v_ref[...] = _dot(mem_n, _wslice(wkv_ref, d, d)).astype(_BF16)

    x = x_ref[...]
    h = _rmsnorm(x, nx_ref[...]).astype(_BF16)
    q = (_dot(h, _wslice(wq_ref, 0, d)) * (hd ** -0.5)).astype(_BF16)
    heads = []
    for i in range(XATTN_HEADS):
        sl = slice(i * hd, (i + 1) * hd)
        s = lax.dot_general(q[:, sl], k_ref[:, sl], (((1,), (1,)), ((), ())), preferred_element_type=_F32)
        e = jnp.exp(s - jnp.max(s, axis=-1, keepdims=True))
        p = e / jnp.sum(e, axis=-1, keepdims=True)
        heads.append(_dot(p.astype(_BF16), v_ref[:, sl]).astype(_BF16))
    o_ref[...] = x + _dot(jnp.concatenate(heads, axis=1), _wslice(wxo_ref, 0, d))


def _hidden_chunks(f):
    tiles = f // MXU_TILE
    bounds = [MXU_TILE * (tiles * c // FFN_HIDDEN_CHUNKS) for c in range(FFN_HIDDEN_CHUNKS)] + [f]
    return list(zip(bounds[:-1], bounds[1:]))


def _ffn_kernel(x_ref, nf_ref, wgu_ref, wdn_ref, nfin_ref, o_ref, *, apply_final_norm):
    seq_tile, d = x_ref.shape
    f = wdn_ref.shape[1]
    sub = seq_tile // FFN_SUB_TILES
    for st in range(FFN_SUB_TILES):
        rsl = slice(st * sub, (st + 1) * sub)
        x = x_ref[rsl, :]
        h = _rmsnorm(x, nf_ref[...]).astype(_BF16)
        y = x
        for lo, hi in _hidden_chunks(f):
            gt = _dot(h, _wslice(wgu_ref, lo, hi - lo))
            up = _dot(h, _wslice(wgu_ref, f + lo, hi - lo))
            y = y + _dot((jax.nn.silu(gt) * up).astype(_BF16), _wslice(wdn_ref, 0, d, lo, hi - lo))
        if apply_final_norm:
            y = _rmsnorm(y, nfin_ref[...])
        o_ref[rsl, :] = y


def _resident(shape):
    return pl.BlockSpec(shape, lambda b, s: (0,) * len(shape), pipeline_mode=pl.Buffered(1))


def _seq_tiled_call(name, body, x, seq_tile, resident_args, extra_specs=(), extra_args=(), scratch_shapes=()):
    batch, seq, d = x.shape
    assert seq % seq_tile == 0
    x_spec = pl.BlockSpec((None, seq_tile, d), lambda b, s: (b, s, 0))
    return pl.pallas_call(
        body,
        name=name,
        out_shape=jax.ShapeDtypeStruct(x.shape, x.dtype),
        grid=(batch, seq // seq_tile),
        in_specs=[x_spec, *extra_specs, *[_resident(a.shape) for a in resident_args]],
        out_specs=x_spec,
        scratch_shapes=list(scratch_shapes),
        compiler_params=pltpu.CompilerParams(
            dimension_semantics=("parallel", "arbitrary"), vmem_limit_bytes=VMEM_LIMIT_BYTES),
    )(x, *extra_args, *resident_args)


def _row(v):
    return v.reshape(1, -1)


def kernel(x, mem, norm_mix, w_in, b_gate, conv_w, conv_b, conv_ln_g, conv_ln_b, w_conv_out, sgu_ln_g, sgu_ln_b, sgu_w, sgu_b, w_sgu_out, w_mix_out, norm_xattn, norm_mem, w_q, w_kv, w_xo, norm_ffn, w_gu, w_down, norm_final):
    batch, seq, d = x.shape
    mem_len = mem.shape[1]
    depth = w_in.shape[0]
    tm = SEQ_TILE_MIXER
    assert depth >= 1
    assert d % (SGU_GROUPS * LANES) == 0 and d % (XATTN_HEADS * LANES) == 0 and d % COL_CHUNK == 0
    assert tm % SGU_CHUNK == 0 and tm % CONV_ROWS == 0 and CONV_ROWS % ROW_CHUNK == 0
    assert w_down.shape[1] % LANES == 0 and SEQ_TILE_FFN % (FFN_SUB_TILES * ROW_CHUNK) == 0
    lb = _lane_blocked

    for l in range(depth):
        x = _seq_tiled_call(
            "mixer", _mixer_kernel, x, tm,
            [_row(norm_mix[l]), lb(w_in[l]), b_gate[l], conv_w[l], _row(conv_b[l]), _row(conv_ln_g[l]),
             _row(conv_ln_b[l]), lb(w_conv_out[l]), _row(sgu_ln_g[l]), _row(sgu_ln_b[l]), sgu_w[l],
             jnp.transpose(sgu_b[l]), lb(w_sgu_out[l]), lb(w_mix_out[l])],
            scratch_shapes=[pltpu.VMEM((tm, d), _BF16),
                            pltpu.VMEM((CONV_HALO + tm, d), _F32),
                            pltpu.VMEM((tm, d), _F32),
                            pltpu.VMEM((tm, d), _BF16),
                            pltpu.VMEM((4, tm, d), _F32),
                            pltpu.VMEM((tm, d), _F32),
                            pltpu.VMEM((tm, d), _BF16),
                            pltpu.VMEM((tm, d), _BF16),
                            pltpu.VMEM((tm, d), _F32),
                            pltpu.VMEM((tm, d), _BF16)])
        x = _seq_tiled_call(
            "xattn", _xattn_kernel, x, SEQ_TILE_XATTN,
            [_row(norm_xattn[l]), _row(norm_mem[l]), lb(w_q[l]), lb(w_kv[l]), lb(w_xo[l])],
            extra_specs=[pl.BlockSpec((None, mem_len, d), lambda b, s: (b, 0, 0))], extra_args=[mem],
            scratch_shapes=[pltpu.VMEM((mem_len, d), _BF16), pltpu.VMEM((mem_len, d), _BF16)])
        x = _seq_tiled_call(
            "ffn", functools.partial(_ffn_kernel, apply_final_norm=(l == depth - 1)), x, SEQ_TILE_FFN,
            [_row(norm_ffn[l]), lb(w_gu[l]), lb(w_down[l]), _row(norm_final)])
    return x
```
